```python
import math
import jax
import jax.numpy as jnp
from jax import lax
import numpy as np

D_MODEL = 4096
BATCH = 4
SEQ = 4096
DEPTH = 1

GRID_W = 64
CTX_LEN = 256
NORM_EPS = 1e-6
N_MOD = 6

D_MIX = D_MODEL
DA_HEAD_DIM = 128
DA_WIDTH = D_MIX // 2
DA_HEADS = DA_WIDTH // (2 * DA_HEAD_DIM)
ROPE_THETA = 10000.0
Q_BLOCK = 128
RW_HEAD = 64
RW_WIDTH = D_MIX - DA_WIDTH
RW_HEADS = RW_WIDTH // RW_HEAD
LORA_W = 128
LORA_A = 128
LORA_G = 480
GN_EPS = 64e-5
RW_COLS = 3 * RW_WIDTH + LORA_W + LORA_A + LORA_G
RW_SPLITS = (RW_WIDTH, 2 * RW_WIDTH, 3 * RW_WIDTH, 3 * RW_WIDTH + LORA_W, 3 * RW_WIDTH + LORA_W + LORA_A)
IN_COLS = 3 * DA_WIDTH + RW_COLS
N_EXPERTS = 64
TOP_K = 6
N_GROUPS = 8
TOPK_GROUPS = 4
EXPERT_FF = 512
SHARED_FF = 512
ROUTED_SCALE = 2.5
EXPERT_BLOCK = 128

kernel_name = "hybrid_diffattn_rwkv7_moe_dit_layer"


def rms_norm(x, g):
    xf = x.astype(jnp.float32)
    y = xf * lax.rsqrt(jnp.mean(xf * xf, axis=-1, keepdims=True) + NORM_EPS)
    return (y * g).astype(x.dtype)


def modulation(cv, w_mod, b_mod, n_chunks):
    n = n_chunks * D_MODEL
    m = jax.nn.silu(cv)[..., None, :] @ w_mod[:, :n] + b_mod[:n]
    return jnp.split(m, n_chunks, axis=-1)


def modulate(h, shift, scale):
    return h * (1.0 + scale) + shift


def axial_rope_tables(rows):
    row = jnp.repeat(jnp.arange(rows, dtype=jnp.float32), GRID_W)
    col = jnp.tile(jnp.arange(GRID_W, dtype=jnp.float32), rows)
    n_pairs = DA_HEAD_DIM // 4
    inv = jnp.power(jnp.float32(ROPE_THETA), -jnp.arange(n_pairs, dtype=jnp.float32) / n_pairs)
    ang = jnp.concatenate([row[:, None] * inv, col[:, None] * inv], axis=-1)
    return jnp.cos(ang), jnp.sin(ang)


def apply_rope(x, cos, sin):
    xp = x.reshape(*x.shape[:-1], -1, 2)
    x0, x1 = xp[..., 0], xp[..., 1]
    cb, sb = cos[None, :, None, None, :], sin[None, :, None, None, :]
    out = jnp.stack([x0 * cb - x1 * sb, x0 * sb + x1 * cb], axis=-1)
    return out.reshape(x.shape).astype(x.dtype)


def da_heads(p):
    q, k, v = jnp.split(p, 3, axis=-1)
    B, L = p.shape[:2]
    q = q.reshape(B, L, DA_HEADS, 2, DA_HEAD_DIM)
    k = k.reshape(B, L, DA_HEADS, 2, DA_HEAD_DIM)
    v = v.reshape(B, L, DA_HEADS, 2 * DA_HEAD_DIM)
    return q, k, v


def lambda_full(lam_vecs, lam_init):
    lv = lam_vecs.astype(jnp.float32)
    return jnp.exp(jnp.sum(lv[0] * lv[1])) - jnp.exp(jnp.sum(lv[2] * lv[3])) + lam_init


def diff_attend(q, k, v, lam):
    s = jnp.einsum('bqhmd,bkhmd->bhmqk', q, k).astype(jnp.float32) * (DA_HEAD_DIM ** -0.5)
    p = jax.nn.softmax(s, axis=-1)
    a = p[:, :, 0] - lam * p[:, :, 1]
    return jnp.einsum('bhqk,bkhe->bqhe', a.astype(v.dtype), v)


def block_sweep_attention(q, k, v, lam):
    B, S = q.shape[:2]
    nblk = S // Q_BLOCK
    qb = jnp.moveaxis(q.reshape(B, nblk, Q_BLOCK, *q.shape[2:]), 1, 0)
    ob = lax.map(lambda qq: diff_attend(qq, k, v, lam), qb)
    return jnp.moveaxis(ob, 0, 1).reshape(B, S, DA_HEADS, 2 * DA_HEAD_DIM)


def da_output(o, subln, lam_init):
    B, L = o.shape[:2]
    return (rms_norm(o, subln) * (1.0 - lam_init)).reshape(B, L, DA_WIDTH)


def centred_shift(p, mu_prev, mu_next):
    prev = jnp.pad(p[:, :-1], ((0, 0), (1, 0), (0, 0)))
    nxt = jnp.pad(p[:, 1:], ((0, 0), (0, 1), (0, 0)))
    return p + mu_prev * (prev - p) + mu_next * (nxt - p)


def rwkv_features(p_rw, k_k, k_a, w0, w2, a0, a2):
    r, k, v, xw, xa, xg = jnp.split(p_rw, RW_SPLITS, axis=-1)
    B, L = r.shape[:2]
    heads = lambda t: t.reshape(B, L, RW_HEADS, RW_HEAD)
    kk = heads(k * k_k).astype(jnp.float32)
    kk = kk / jnp.maximum(jnp.sqrt(jnp.sum(kk * kk, axis=-1, keepdims=True)), 1e-12)
    per_dir = []
    for d in range(2):
        wlog = -jax.nn.softplus(-(w0[d] + jnp.tanh(xw) @ w2[d])) - 0.5
        decay = jnp.exp(-jnp.exp(wlog.astype(jnp.float32)))
        a = jax.nn.sigmoid(a0[d] + xa @ a2[d])
        kd = k * (1.0 + (a - 1.0) * k_a)
        per_dir.append((heads(decay), heads(a), heads(kd)))
    return heads(r), heads(v), kk, xg, per_dir


def wkv7_scan(state0, decay, a, k, v, kk, r, reverse):
    seq_first = lambda t: jnp.moveaxis(t.astype(jnp.float32), 1, 0)
    xs = (seq_first(decay), seq_first(a), seq_first(k), seq_first(v), seq_first(kk))
    if r is not None:
        xs = xs + (seq_first(r),)

    def step(S, inp):
        w_t, a_t, k_t, v_t, kk_t = inp[:5]
        sa = jnp.einsum('bhvk,bhk->bhv', S, -kk_t)
        S = (S * w_t[:, :, None, :] + sa[..., None] * (kk_t * a_t)[:, :, None, :]
             + v_t[..., None] * k_t[:, :, None, :])
        y = None if r is None else jnp.einsum('bhvk,bhk->bhv', S, inp[5])
        return S, y

    S, ys = lax.scan(step, state0, xs, reverse=reverse)
    return S, (None if r is None else jnp.moveaxis(ys, 0, 1))


def rwkv_output(y, r, v, per_dir, r_k, ln_w, ln_b, xg, g2):
    B, L = y.shape[:2]
    mu = jnp.mean(y, axis=-1, keepdims=True)
    var = jnp.mean(jnp.square(y - mu), axis=-1, keepdims=True)
    yn = ((y - mu) * lax.rsqrt(var + GN_EPS)).reshape(B, L, RW_WIDTH) * ln_w + ln_b
    rf, vf = r.astype(jnp.float32), v.astype(jnp.float32)
    bonus = sum(jnp.sum(rf * kd.astype(jnp.float32) * r_k, axis=-1, keepdims=True) * vf
                for (_, _, kd) in per_dir)
    g = jax.nn.sigmoid(xg) @ g2
    return (yn + bonus.reshape(B, L, RW_WIDTH)) * g


def swiglu(t, w1, w3, w2):
    return (jax.nn.silu(t @ w1) * (t @ w3)) @ w2


def moe_ffn(h, router_w, router_bias, w1, w3, w2, sw1, sw3, sw2):
    B, L, D = h.shape
    T = B * L
    t = h.reshape(T, D)
    scores = jax.nn.sigmoid((t @ router_w).astype(jnp.float32))
    biased = scores + router_bias.astype(jnp.float32)
    grp = biased.reshape(T, N_GROUPS, N_EXPERTS // N_GROUPS)
    grp_score = jnp.sum(lax.top_k(grp, 2)[0], axis=-1)
    _, grp_idx = lax.top_k(grp_score, TOPK_GROUPS)
    grp_mask = jnp.sum(jax.nn.one_hot(grp_idx, N_GROUPS, dtype=jnp.float32), axis=1) > 0
    exp_mask = jnp.repeat(grp_mask, N_EXPERTS // N_GROUPS, axis=1)
    _, e_idx = lax.top_k(jnp.where(exp_mask, biased, -jnp.inf), TOP_K)
    gates = jnp.take_along_axis(scores, e_idx, axis=1)
    gates = gates / jnp.sum(gates, axis=-1, keepdims=True) * ROUTED_SCALE
    TK = T * TOP_K
    nb = -(-TK // EXPERT_BLOCK) + N_EXPERTS
    P = nb * EXPERT_BLOCK
    e_flat = e_idx.reshape(-1).astype(jnp.int32)
    tok_flat = jnp.repeat(jnp.arange(T, dtype=jnp.int32), TOP_K)
    w_flat = gates.reshape(-1)
    order = jnp.argsort(e_flat)
    e_s, tok_s, w_s = e_flat[order], tok_flat[order], w_flat[order]
    counts = jnp.bincount(e_flat, length=N_EXPERTS).astype(jnp.int32)
    starts = jnp.cumsum(counts) - counts
    padded = ((counts + EXPERT_BLOCK - 1) // EXPERT_BLOCK) * EXPERT_BLOCK
    pends = jnp.cumsum(padded)
    pstarts = pends - padded
    dest = pstarts[e_s] + (jnp.arange(TK, dtype=jnp.int32) - starts[e_s])
    tok_buf = jnp.full((P,), T, jnp.int32).at[dest].set(tok_s)
    w_buf = jnp.zeros((P,), t.dtype).at[dest].set(w_s.astype(t.dtype))
    blk_expert = jnp.clip(jnp.searchsorted(pends, jnp.arange(nb, dtype=jnp.int32) * EXPERT_BLOCK,
                                           side='right'), 0, N_EXPERTS - 1).astype(jnp.int32)
    t_pad = jnp.concatenate([t, jnp.zeros((1, D), t.dtype)], axis=0)

    def expert_block(acc, blk):
        e, toks, wv = blk
        xb = t_pad[toks]
        hb = jax.nn.silu(xb @ w1[e]) * (xb @ w3[e])
        return acc.at[toks].add(((hb @ w2[e]) * wv[:, None]).astype(acc.dtype)), None

    acc, _ = lax.scan(expert_block, jnp.zeros((T + 1, D), t.dtype),
                      (blk_expert, tok_buf.reshape(nb, EXPERT_BLOCK), w_buf.reshape(nb, EXPERT_BLOCK)))
    routed = acc[:T]
    shared = swiglu(t, sw1, sw3, sw2)
    return (routed + shared).reshape(B, L, D)


def setup_inputs(seed: int = 0) -> dict:
    key = jax.random.key(seed)
    ks = iter(jax.random.split(key, 48))
    L, D, E = DEPTH, D_MODEL, N_EXPERTS

    def nrm(shape, scale):
        return jax.random.normal(next(ks), shape, jnp.float32) * scale

    def gain(shape):
        return 1.0 + nrm(shape, 0.05)

    def unif(shape, lo, hi):
        return jax.random.uniform(next(ks), shape, jnp.float32, lo, hi)

    return {
        "x": nrm((BATCH, SEQ, D), 1.0),
        "c": nrm((BATCH, D), 1.0),
        "ctx": nrm((BATCH, CTX_LEN, D), 1.0),
        "c_ctx": nrm((D,), 1.0),
        "w_mod": nrm((L, D, N_MOD * D), 0.5 * D ** -0.5),
        "b_mod": nrm((L, N_MOD * D), 0.02),
        "g_pre_attn": gain((L, D)),
        "g_post_attn": gain((L, D)),
        "g_pre_ffn": gain((L, D)),
        "g_post_ffn": gain((L, D)),
        "w_in": nrm((L, D, IN_COLS), D ** -0.5),
        "w_out": nrm((L, D_MIX, D), D_MIX ** -0.5),
        "da_lambda": nrm((L, 4, DA_HEAD_DIM), 0.1),
        "da_subln": gain((L, 2 * DA_HEAD_DIM)),
        "rw_shift": unif((L, 2, RW_COLS), 0.0, 0.5),
        "rw_k_k": 0.85 + nrm((L, RW_WIDTH), 0.05),
        "rw_k_a": gain((L, RW_WIDTH)),
        "rw_r_k": nrm((L, RW_HEADS, RW_HEAD), 0.1),
        "rw_w0": unif((L, 2, RW_WIDTH), -6.0, -1.0),
        "rw_w2": nrm((L, 2, LORA_W, RW_WIDTH), 0.1 * LORA_W ** -0.5),
        "rw_a0": nrm((L, 2, RW_WIDTH), 0.1),
        "rw_a2": nrm((L, 2, LORA_A, RW_WIDTH), 0.5 * LORA_A ** -0.5),
        "rw_g2": nrm((L, LORA_G, RW_WIDTH), LORA_G ** -0.5),
        "rw_ln_w": gain((L, RW_WIDTH)),
        "rw_ln_b": nrm((L, RW_WIDTH), 0.02),
        "router_w": nrm((L, D, E), D ** -0.5),
        "router_bias": nrm((L, E), 0.01),
        "exp_w1": nrm((L, E, D, EXPERT_FF), D ** -0.5),
        "exp_w3": nrm((L, E, D, EXPERT_FF), D ** -0.5),
        "exp_w2": nrm((L, E, EXPERT_FF, D), EXPERT_FF ** -0.5),
        "sh_w1": nrm((L, D, SHARED_FF), D ** -0.5),
        "sh_w3": nrm((L, D, SHARED_FF), D ** -0.5),
        "sh_w2": nrm((L, SHARED_FF, D), SHARED_FF ** -0.5),
    }


def reference(x, c, ctx, c_ctx, w_mod, b_mod, g_pre_attn, g_post_attn, g_pre_ffn, g_post_ffn,
              w_in, w_out, da_lambda, da_subln, rw_shift, rw_k_k, rw_k_a, rw_r_k,
              rw_w0, rw_w2, rw_a0, rw_a2, rw_g2, rw_ln_w, rw_ln_b,
              router_w, router_bias, exp_w1, exp_w3, exp_w2, sh_w1, sh_w3, sh_w2):
    B, S = x.shape[:2]
    ROWS = S // GRID_W
    cos, sin = axial_rope_tables(ROWS)
    for l in range(DEPTH):
        update_ctx = l < DEPTH - 1
        mod = modulation(c, w_mod[l], b_mod[l], N_MOD)
        mod_c = modulation(c_ctx, w_mod[l], b_mod[l], N_MOD if update_ctx else 2)

        u_lat = modulate(rms_norm(x, g_pre_attn[l]), mod[0], mod[1])
        u_ctx = modulate(rms_norm(ctx, g_pre_attn[l]), mod_c[0], mod_c[1])
        p_lat = u_lat @ w_in[l]
        p_ctx = u_ctx @ w_in[l]

        lam_init = 0.8 - 0.6 * math.exp(-0.3 * l)
        lam = lambda_full(da_lambda[l], lam_init)
        q_l, k_l, v_l = da_heads(p_lat[..., :3 * DA_WIDTH])
        q_c, k_c, v_c = da_heads(p_ctx[..., :3 * DA_WIDTH])
        q_l = apply_rope(q_l, cos, sin)
        k_l = apply_rope(k_l, cos, sin)
        k_all = jnp.concatenate([k_c, k_l], axis=1)
        v_all = jnp.concatenate([v_c, v_l], axis=1)
        o_da = da_output(block_sweep_attention(q_l, k_all, v_all, lam), da_subln[l], lam_init)

        rw_lat = centred_shift(p_lat[..., 3 * DA_WIDTH:], rw_shift[l, 0], rw_shift[l, 1])
        rw_ctx = centred_shift(p_ctx[..., 3 * DA_WIDTH:], rw_shift[l, 0], rw_shift[l, 1])
        r_l, vr_l, kk_l, xg_l, dirs_l = rwkv_features(rw_lat, rw_k_k[l], rw_k_a[l], rw_w0[l], rw_w2[l], rw_a0[l], rw_a2[l])
        r_c, vr_c, kk_c, xg_c, dirs_c = rwkv_features(rw_ctx, rw_k_k[l], rw_k_a[l], rw_w0[l], rw_w2[l], rw_a0[l], rw_a2[l])
        state0 = jnp.zeros((B, RW_HEADS, RW_HEAD, RW_HEAD), jnp.float32)
        r_c_emit = r_c if update_ctx else None
        s_cf, yc_f = wkv7_scan(state0, *dirs_c[0], vr_c, kk_c, r_c_emit, reverse=False)
        s_cb, yc_b = wkv7_scan(state0, *dirs_c[1], vr_c, kk_c, r_c_emit, reverse=True)
        _, y_f = wkv7_scan(s_cf, *dirs_l[0], vr_l, kk_l, r_l, reverse=False)
        _, y_b = wkv7_scan(s_cb, *dirs_l[1], vr_l, kk_l, r_l, reverse=True)
        o_rw = rwkv_output(y_f + y_b, r_l, vr_l, dirs_l, rw_r_k[l], rw_ln_w[l], rw_ln_b[l], xg_l, rw_g2[l])

        o_lat = jnp.concatenate([o_da.astype(jnp.float32), o_rw.astype(jnp.float32)], axis=-1).astype(x.dtype) @ w_out[l]
        x_new = x + (mod[2] * rms_norm(o_lat, g_post_attn[l])).astype(x.dtype)

        h = modulate(rms_norm(x_new, g_pre_ffn[l]), mod[3], mod[4])
        y_moe = moe_ffn(h, router_w[l], router_bias[l], exp_w1[l], exp_w3[l], exp_w2[l], sh_w1[l], sh_w3[l], sh_w2[l])
        x_new = x_new + (mod[5] * rms_norm(y_moe, g_post_ffn[l])).astype(x.dtype)

        if update_ctx:
            o_da_c = da_output(diff_attend(q_c, k_c, v_c, lam), da_subln[l], lam_init)
            o_rw_c = rwkv_output(yc_f + yc_b, r_c, vr_c, dirs_c, rw_r_k[l], rw_ln_w[l], rw_ln_b[l], xg_c, rw_g2[l])
            o_c = jnp.concatenate([o_da_c.astype(jnp.float32), o_rw_c.astype(jnp.float32)], axis=-1).astype(ctx.dtype) @ w_out[l]
            ctx_new = ctx + (mod_c[2] * rms_norm(o_c, g_post_attn[l])).astype(ctx.dtype)
            h_c = modulate(rms_norm(ctx_new, g_pre_ffn[l]), mod_c[3], mod_c[4])
            y_c = moe_ffn(h_c, router_w[l], router_bias[l], exp_w1[l], exp_w3[l], exp_w2[l], sh_w1[l], sh_w3[l], sh_w2[l])
            ctx = ctx_new + (mod_c[5] * rms_norm(y_c, g_post_ffn[l])).astype(ctx.dtype)
        x = x_new
    return x
```

```python
import functools
import math

import numpy as np
import jax
import jax.numpy as jnp
from jax import lax
from jax.experimental import pallas as pl
from jax.experimental.pallas import tpu as pltpu

F32 = jnp.float32
BF16 = jnp.bfloat16

GRID_W = 64
NORM_EPS = 1e-6
N_MOD = 6
DA_HEAD_DIM = 128
ROPE_THETA = 10000.0
RW_HEAD = 64
GN_EPS = 64e-5
N_GROUPS = 8
TOPK_GROUPS = 4
TOP_K = 6
ROUTED_SCALE = 2.5

LANES = 128
SUBLANES = 8
VMEM_LIMIT = 56 * 1024 * 1024

SCAN_CHUNK = 64
EXPERT_ROWS = 256


def _tile(n, pref, mult):
    best = None
    for t in range(mult, min(n, pref) + 1, mult):
        if n % t == 0:
            best = t
    return n if best is None else best


def _params(*sem):
    return pltpu.CompilerParams(dimension_semantics=sem, vmem_limit_bytes=VMEM_LIMIT)


def _dot(a, b):
    return jnp.dot(a, b, preferred_element_type=F32)


def _dot_nt(a, b):
    return lax.dot_general(a, b, (((1,), (1,)), ((), ())), preferred_element_type=F32)


def _dot_tn(a, b):
    return lax.dot_general(a, b, (((0,), (0,)), ((), ())), preferred_element_type=F32)


def _split(x):
    hi = x.astype(BF16)
    lo = (x - hi.astype(F32)).astype(BF16)
    return hi, lo


def _dot_hilo(x, e):
    hi, lo = _split(x)
    return _dot(hi, e) + _dot(lo, e)


def _sigmoid(x):
    return 1.0 / (1.0 + jnp.exp(-x))


def _mod_kernel(cv_ref, w_ref, b_ref, o_ref):
    cv = cv_ref[...]
    a = (cv * _sigmoid(cv)).astype(BF16)
    o_ref[...] = _dot(a, w_ref[...].astype(BF16)) + b_ref[...]


def _modulation(cv, w_mod, b_mod):
    rows, d = cv.shape
    n = w_mod.shape[1]
    tn = _tile(n, 512, LANES)
    return pl.pallas_call(
        _mod_kernel,
        out_shape=jax.ShapeDtypeStruct((rows, n), F32),
        grid=(n // tn,),
        in_specs=[pl.BlockSpec((rows, d), lambda j: (0, 0)),
                  pl.BlockSpec((d, tn), lambda j: (0, j)),
                  pl.BlockSpec((1, tn), lambda j: (0, j))],
        out_specs=pl.BlockSpec((rows, tn), lambda j: (0, j)),
        compiler_params=_params("parallel"),
        name="modulation",
    )(cv, w_mod, b_mod.reshape(1, n))


def _rms(xv, g):
    ms = jnp.mean(xv * xv, axis=-1, keepdims=True)
    return xv * lax.rsqrt(ms + NORM_EPS) * g


def _prenorm_kernel(ctx_ref, x_ref, g_ref, sh_ref, sc_ref, shc_ref, scc_ref, o_ref, *, nctx):
    j = pl.program_id(1)

    @pl.when(j < nctx)
    def _():
        o_ref[0] = (_rms(ctx_ref[0], g_ref[...]) * (1.0 + scc_ref[0]) + shc_ref[0]).astype(o_ref.dtype)

    @pl.when(j >= nctx)
    def _():
        o_ref[0] = (_rms(x_ref[0], g_ref[...]) * (1.0 + sc_ref[0]) + sh_ref[0]).astype(o_ref.dtype)


def _prenorm(ctx, x, g, mod3, tt):
    b, s, d = x.shape
    c = ctx.shape[1]
    nctx = c // tt
    nb = (c + s) // tt
    return pl.pallas_call(
        functools.partial(_prenorm_kernel, nctx=nctx),
        out_shape=jax.ShapeDtypeStruct((b, c + s, d), BF16),
        grid=(b, nb),
        in_specs=[pl.BlockSpec((1, tt, d), lambda i, j: (i, jnp.minimum(j, nctx - 1), 0)),
                  pl.BlockSpec((1, tt, d), lambda i, j: (i, jnp.maximum(j - nctx, 0), 0)),
                  pl.BlockSpec((1, d), lambda i, j: (0, 0)),
                  pl.BlockSpec((1, 1, d), lambda i, j: (i, 0, 0)),
                  pl.BlockSpec((1, 1, d), lambda i, j: (i, 0, 1)),
                  pl.BlockSpec((1, 1, d), lambda i, j: (b, 0, 0)),
                  pl.BlockSpec((1, 1, d), lambda i, j: (b, 0, 1))],
        out_specs=pl.BlockSpec((1, tt, d), lambda i, j: (i, j, 0)),
        compiler_params=_params("parallel", "parallel"),
        name="prenorm",
    )(ctx, x, g.reshape(1, d), mod3, mod3, mod3, mod3)


def _mm_kernel(a_ref, b_ref, o_ref):
    o_ref[...] = _dot(a_ref[...], b_ref[...]).astype(o_ref.dtype)


def _mm_rope_kernel(a_ref, b_ref, c_ref, s_ref, o_ref):
    acc = _dot(a_ref[...], b_ref[...])
    cos = c_ref[...]
    sin = s_ref[...]
    for c in range(acc.shape[1] // DA_HEAD_DIM):
        sl = slice(c * DA_HEAD_DIM, (c + 1) * DA_HEAD_DIM)
        blk = acc[:, sl]
        o_ref[:, sl] = (blk * cos + pltpu.roll(blk, DA_HEAD_DIM // 2, 1) * sin).astype(o_ref.dtype)


def _mm2_kernel(a1_ref, a2_ref, b1_ref, b2_ref, o_ref):
    o_ref[...] = (_dot(a1_ref[...], b1_ref[...]) + _dot(a2_ref[...], b2_ref[...])).astype(o_ref.dtype)


def _matmul(a, w, rope=None):
    m, k = a.shape
    n = w.shape[1]
    tm = _tile(m, 1024, 16)
    tn = _tile(n, 1152, LANES)
    in_specs = [pl.BlockSpec((tm, k), lambda i, j: (i, 0)),
                pl.BlockSpec((k, tn), lambda i, j: (0, j))]
    args = [a, w]
    kern = _mm_kernel
    if rope is not None:
        in_specs += [pl.BlockSpec((tm, DA_HEAD_DIM), lambda i, j: (i, 0))] * 2
        args += list(rope)
        kern = _mm_rope_kernel
    return pl.pallas_call(
        kern,
        out_shape=jax.ShapeDtypeStruct((m, n), BF16),
        grid=(m // tm, n // tn),
        in_specs=in_specs,
        out_specs=pl.BlockSpec((tm, tn), lambda i, j: (i, j)),
        compiler_params=_params("parallel", "parallel"),
        name="proj_rope" if rope is not None else "proj",
    )(*args)


def _matmul2(a1, a2, w):
    m, k1 = a1.shape
    k2 = a2.shape[1]
    assert k1 == k2
    n = w.shape[1]
    tm = _tile(m, 1024, 16)
    tn = _tile(n, 1024, LANES)
    return pl.pallas_call(
        _mm2_kernel,
        out_shape=jax.ShapeDtypeStruct((m, n), BF16),
        grid=(m // tm, n // tn),
        in_specs=[pl.BlockSpec((tm, k1), lambda i, j: (i, 0)),
                  pl.BlockSpec((tm, k2), lambda i, j: (i, 0)),
                  pl.BlockSpec((k1, tn), lambda i, j: (0, j)),
                  pl.BlockSpec((k2, tn), lambda i, j: (1, j))],
        out_specs=pl.BlockSpec((tm, tn), lambda i, j: (i, j)),
        compiler_params=_params("parallel", "parallel"),
        name="out_proj",
    )(a1, a2, w, w)


def _attn_kernel(lam_ref, q_ref, k_ref, v_ref, g_ref, o_ref):
    lam = lam_ref[0]
    q = q_ref[0]
    k = k_ref[0]
    v = v_ref[0]
    outs = []
    for m in range(2):
        sl = slice(m * DA_HEAD_DIM, (m + 1) * DA_HEAD_DIM)
        s = _dot_nt(q[:, sl], k[:, sl])
        p = jnp.exp(s - jnp.max(s, axis=-1, keepdims=True))
        l = jnp.sum(p, axis=-1, keepdims=True)
        outs.append(_dot(p.astype(BF16), v) / l)
    o = outs[0] - lam * outs[1]
    o_ref[0] = _rms(o, g_ref[...]).astype(o_ref.dtype)


def _diff_attention(lam, q, k, v, g_sub, n_ctx_rows, tq):
    b, lc, w = q.shape
    hw = 2 * DA_HEAD_DIM
    nh = w // hw
    s = lc - n_ctx_rows
    off = n_ctx_rows // tq
    return pl.pallas_call(
        _attn_kernel,
        out_shape=jax.ShapeDtypeStruct((b, s, w), BF16),
        grid=(b, nh, s // tq),
        in_specs=[pl.BlockSpec(memory_space=pltpu.SMEM),
                  pl.BlockSpec((1, tq, hw), lambda i, h, j: (i, j + off, h)),
                  pl.BlockSpec((1, lc, hw), lambda i, h, j: (i, 0, h)),
                  pl.BlockSpec((1, lc, hw), lambda i, h, j: (i, 0, h)),
                  pl.BlockSpec((1, hw), lambda i, h, j: (0, 0))],
        out_specs=pl.BlockSpec((1, tq, hw), lambda i, h, j: (i, j, h)),
        compiler_params=_params("parallel", "parallel", "parallel"),
        name="diff_attention",
    )(lam, q, k, v, g_sub)


def _rw_feature_kernel(p_ref, ph_ref, nh_ref, mu_ref, vec_ref, w0_ref, a0_ref, w2_ref, a2_ref, g2_ref,
                       e_ref, et_ref,
                       r_ref, v_ref, kk_ref, lw0_ref, lw1_ref, b0_ref, b1_ref, kd0_ref, kd1_ref,
                       bonus_ref, g_ref, *, tt, rw, n_ctx_rows, lora_w, lora_a):
    j = pl.program_id(1)
    lc = pl.num_programs(1) * tt
    p = p_ref[0].astype(F32)
    row = lax.broadcasted_iota(jnp.int32, (tt, 1), 0)
    pos = row + j * tt
    prev = jnp.where(row == 0, ph_ref[0, SUBLANES - 1:SUBLANES, :].astype(F32), pltpu.roll(p, 1, 0))
    prev = jnp.where((pos == 0) | (pos == n_ctx_rows), 0.0, prev)
    nxt = jnp.where(row == tt - 1, nh_ref[0, 0:1, :].astype(F32), pltpu.roll(p, tt - 1, 0))
    nxt = jnp.where((pos == n_ctx_rows - 1) | (pos == lc - 1), 0.0, nxt)
    x = p + mu_ref[0:1, :] * (prev - p) + mu_ref[1:2, :] * (nxt - p)

    r = x[:, 0:rw]
    k = x[:, rw:2 * rw]
    v = x[:, 2 * rw:3 * rw]
    c0 = 3 * rw
    xw = x[:, c0:c0 + lora_w]
    xa = x[:, c0 + lora_w:c0 + lora_w + lora_a]
    xg = x[:, c0 + lora_w + lora_a:]
    k_k = vec_ref[0:1, :]
    k_a = vec_ref[1:2, :]
    r_k = vec_ref[2:3, :]
    e = e_ref[...]
    et = et_ref[...]

    kkp = k * k_k
    ss = _dot_hilo(_dot_hilo(kkp * kkp, e), et)
    kk = kkp / jnp.maximum(jnp.sqrt(ss), 1e-12)
    th = jnp.tanh(xw).astype(BF16)
    xab = xa.astype(BF16)
    r_ref[0] = r.astype(r_ref.dtype)
    v_ref[0] = v.astype(v_ref.dtype)
    kk_ref[0] = kk.astype(kk_ref.dtype)
    bsum = jnp.zeros_like(r)
    for d, (lw_ref, b_ref, kd_ref) in enumerate(((lw0_ref, b0_ref, kd0_ref), (lw1_ref, b1_ref, kd1_ref))):
        wl = w0_ref[d:d + 1, :] + _dot(th, w2_ref[d])
        wlog = jnp.minimum(wl, 0.0) - jnp.log(1.0 + jnp.exp(-jnp.abs(wl))) - 0.5
        lw_ref[0] = -jnp.exp(wlog)
        a = _sigmoid(a0_ref[d:d + 1, :] + _dot(xab, a2_ref[d]))
        kd = k * (1.0 + (a - 1.0) * k_a)
        b_ref[0] = (kk * a).astype(b_ref.dtype)
        kd_ref[0] = kd.astype(kd_ref.dtype)
        bsum = bsum + r * kd * r_k
    bonus_ref[0] = (_dot_hilo(_dot_hilo(bsum, e), et) * v).astype(bonus_ref.dtype)
    g_ref[0] = _dot(_sigmoid(xg).astype(BF16), g2_ref[...]).astype(g_ref.dtype)


def _head_indicator(rw):
    nh = rw // RW_HEAD
    cols = max(LANES, nh)
    e = np.zeros((rw, cols), np.float32)
    e[np.arange(rw), np.arange(rw) // RW_HEAD] = 1.0
    return jnp.asarray(e, BF16), jnp.asarray(e.T, BF16)


def _rw_features(p_rw, mu, vecs, w0, a0, w2, a2, g2p, rw, n_ctx_rows, tt):
    b, lc, ncol = p_rw.shape
    lora_w = w2.shape[1]
    lora_a = a2.shape[1]
    lora_gp = g2p.shape[0]
    e, et = _head_indicator(rw)
    nblk = lc // tt
    hb = tt // SUBLANES
    nh8 = lc // SUBLANES
    full = lambda shape: pl.BlockSpec(shape, lambda i, j: (0,) * len(shape))
    tok = lambda dt: jax.ShapeDtypeStruct((b, lc, rw), dt)
    out_shape = [tok(BF16), tok(BF16), tok(BF16), tok(F32), tok(F32),
                 tok(BF16), tok(BF16), tok(BF16), tok(BF16), tok(BF16), tok(BF16)]
    ospec = pl.BlockSpec((1, tt, rw), lambda i, j: (i, j, 0))
    return pl.pallas_call(
        functools.partial(_rw_feature_kernel, tt=tt, rw=rw, n_ctx_rows=n_ctx_rows,
                          lora_w=lora_w, lora_a=lora_a),
        out_shape=out_shape,
        grid=(b, nblk),
        in_specs=[pl.BlockSpec((1, tt, ncol), lambda i, j: (i, j, 0)),
                  pl.BlockSpec((1, SUBLANES, ncol), lambda i, j: (i, jnp.maximum(j * hb - 1, 0), 0)),
                  pl.BlockSpec((1, SUBLANES, ncol), lambda i, j: (i, jnp.minimum((j + 1) * hb, nh8 - 1), 0)),
                  full((2, ncol)), full((3, rw)), full((2, rw)), full((2, rw)),
                  full((2, lora_w, rw)), full((2, lora_a, rw)), full((lora_gp, rw)),
                  full(e.shape), full(et.shape)],
        out_specs=[ospec] * len(out_shape),
        compiler_params=_params("parallel", "parallel"),
        name="rwkv_features",
    )(p_rw, p_rw, p_rw, mu, vecs, w0, a0, w2, a2, g2p, e, et)


def _scan_kernel(r_ref, v_ref, kk_ref, lw_ref, b_ref, kd_ref, y_ref, h_ref, *, reverse, heads, chunk):
    c = chunk
    n = RW_HEAD

    @pl.when(pl.program_id(2) == 0)
    def _():
        h_ref[...] = jnp.zeros_like(h_ref)

    row = lax.broadcasted_iota(jnp.int32, (c, c), 0)
    col = lax.broadcasted_iota(jnp.int32, (c, c), 1)
    if reverse:
        strict, incl = col > row, col >= row
    else:
        strict, incl = col < row, col <= row
    eye_c = (row == col).astype(F32)
    eye_n = lax.broadcasted_iota(jnp.int32, (n, n), 0) == lax.broadcasted_iota(jnp.int32, (n, n), 1)

    lw = lw_ref[0]
    cum = jnp.dot(incl.astype(F32), lw, preferred_element_type=F32, precision=lax.Precision.HIGHEST)
    tot = jnp.sum(lw, axis=0, keepdims=True)
    e_neg = jnp.exp(-cum)
    e_rem = jnp.exp(tot - cum)
    kk = kk_ref[0].astype(F32)
    bb = b_ref[0].astype(F32)
    kd = kd_ref[0].astype(F32)
    a_t = (-kk * jnp.exp(cum - lw)).astype(BF16)
    r_t = (r_ref[0].astype(F32) * jnp.exp(cum)).astype(BF16)
    b_t = (bb * e_neg).astype(BF16)
    k_t = (kd * e_neg).astype(BF16)
    b_h = (bb * e_rem).astype(BF16)
    k_h = (kd * e_rem).astype(BF16)
    g_tot = jnp.exp(tot)
    vv = v_ref[0]

    for g in range(heads):
        sl = slice(g * n, (g + 1) * n)
        at, rt, bt, kt = a_t[:, sl], r_t[:, sl], b_t[:, sl], k_t[:, sl]
        v = vv[:, sl]
        gram = _dot_nt(jnp.concatenate([at, rt], axis=0), jnp.concatenate([bt, kt], axis=0))
        a_ab = jnp.where(strict, gram[:c, :c], 0.0)
        a_ak = jnp.where(strict, gram[:c, c:], 0.0).astype(BF16)
        a_rb = jnp.where(incl, gram[c:, :c], 0.0).astype(BF16)
        a_rk = jnp.where(incl, gram[c:, c:], 0.0).astype(BF16)
        t_inv = eye_c + a_ab
        pw = a_ab
        for _ in range(int(math.log2(c)) - 1):
            pw = jnp.dot(pw, pw, preferred_element_type=F32, precision=lax.Precision.HIGHEST)
            t_inv = t_inv + jnp.dot(t_inv, pw, preferred_element_type=F32, precision=lax.Precision.HIGHEST)
        akv = _dot(a_ak, v)
        wu = _dot(t_inv.astype(BF16), jnp.concatenate([at, akv.astype(BF16)], axis=1)).astype(BF16)
        rb = _dot(a_rb, wu)
        q_hat = rt.astype(F32) + rb[:, :n]
        y_loc = rb[:, n:] + _dot(a_rk, v)
        bw = _dot_tn(b_h[:, sl], wu)
        m_mat = bw[:, :n] + jnp.where(eye_n, g_tot[:, sl], 0.0)
        n_mat = bw[:, n:] + _dot_tn(k_h[:, sl], v)
        h_hi, h_lo = _split(h_ref[g])
        qm = jnp.concatenate([q_hat, m_mat], axis=0)
        qm_hi, qm_lo = _split(qm)
        res = _dot(qm_hi, h_hi) + _dot(qm_hi, h_lo) + _dot(qm_lo, h_hi)
        y_ref[0, :, sl] = (res[:c] + y_loc).astype(y_ref.dtype)
        h_ref[g] = res[c:] + n_mat


def _rw_scan(r, v, kk, lw, bb, kd, n_ctx_rows, reverse):
    b, lc, rw = r.shape
    c = SCAN_CHUNK
    nc = lc // c
    nctx = n_ctx_rows // c
    nlat = nc - nctx
    heads = min(8, rw // RW_HEAD)
    gw = heads * RW_HEAD
    if reverse:
        cidx = lambda j: jnp.where(j < nctx, nctx - 1 - j, nc - 1 - (j - nctx))
        oidx = lambda j: jnp.where(j < nctx, nlat - 1, nlat - 1 - (j - nctx))
    else:
        cidx = lambda j: j
        oidx = lambda j: jnp.maximum(j - nctx, 0)
    ispec = pl.BlockSpec((1, c, gw), lambda i, h, j: (i, cidx(j), h))
    return pl.pallas_call(
        functools.partial(_scan_kernel, reverse=reverse, heads=heads, chunk=c),
        out_shape=jax.ShapeDtypeStruct((b, lc - n_ctx_rows, rw), BF16),
        grid=(b, rw // gw, nc),
        in_specs=[ispec] * 6,
        out_specs=pl.BlockSpec((1, c, gw), lambda i, h, j: (i, oidx(j), h)),
        scratch_shapes=[pltpu.VMEM((heads, RW_HEAD, RW_HEAD), F32)],
        compiler_params=_params("parallel", "parallel", "arbitrary"),
        name="rwkv_scan_bwd" if reverse else "rwkv_scan_fwd",
    )(r, v, kk, lw, bb, kd)


def _rw_out_kernel(yf_ref, yb_ref, bonus_ref, g_ref, ln_ref, e_ref, et_ref, o_ref):
    e = e_ref[...]
    et = et_ref[...]
    y = yf_ref[0].astype(F32) + yb_ref[0].astype(F32)
    inv = 1.0 / RW_HEAD
    mu = _dot_hilo(_dot_hilo(y, e) * inv, et)
    d = y - mu
    var = _dot_hilo(_dot_hilo(d * d, e) * inv, et)
    yn = d * lax.rsqrt(var + GN_EPS) * ln_ref[0:1, :] + ln_ref[1:2, :]
    o_ref[0] = ((yn + bonus_ref[0].astype(F32)) * g_ref[0].astype(F32)).astype(o_ref.dtype)


def _rw_output(yf, yb, bonus, g, ln, n_ctx_rows, tt):
    b, s, rw = yf.shape
    e, et = _head_indicator(rw)
    off = n_ctx_rows // tt
    lat = pl.BlockSpec((1, tt, rw), lambda i, j: (i, j, 0))
    allrows = pl.BlockSpec((1, tt, rw), lambda i, j: (i, j + off, 0))
    full = lambda shape: pl.BlockSpec(shape, lambda i, j: (0,) * len(shape))
    return pl.pallas_call(
        _rw_out_kernel,
        out_shape=jax.ShapeDtypeStruct((b, s, rw), BF16),
        grid=(b, s // tt),
        in_specs=[lat, lat, allrows, allrows, full((2, rw)), full(e.shape), full(et.shape)],
        out_specs=lat,
        compiler_params=_params("parallel", "parallel"),
        name="rwkv_output",
    )(yf, yb, bonus, g, ln, e, et)


def _post_attn_kernel(x_ref, o_ref, gate_ref, sh_ref, sc_ref, gpost_ref, gpre_ref, rwt_ref, rb_ref,
                      xn_ref, h_ref, idx_ref, gates_ref, *, n_exp):
    xn = x_ref[0] + gate_ref[0] * _rms(o_ref[0].astype(F32), gpost_ref[...])
    xn_ref[0] = xn
    h = _rms(xn, gpre_ref[...]) * (1.0 + sc_ref[0]) + sh_ref[0]
    h_ref[0] = h

    h_hi, h_lo = _split(h)
    w_hi, w_lo = _split(rwt_ref[...])
    logits = _dot_nt(w_hi, h_hi) + _dot_nt(w_lo, h_hi) + _dot_nt(w_hi, h_lo)
    scores = _sigmoid(logits)
    biased = scores + rb_ref[...]
    tt = scores.shape[1]
    gsz = n_exp // N_GROUPS
    neg = -jnp.inf

    def first_argmax(vals, iota, big):
        m = jnp.max(vals, axis=0, keepdims=True)
        return jnp.min(jnp.where(vals == m, iota, big), axis=0, keepdims=True), m

    iota_g = lax.broadcasted_iota(jnp.int32, (gsz, tt), 0)
    grp_rows = []
    for g in range(N_GROUPS):
        sub = biased[g * gsz:(g + 1) * gsz, :]
        i1, m1 = first_argmax(sub, iota_g, gsz)
        m2 = jnp.max(jnp.where(iota_g == i1, neg, sub), axis=0, keepdims=True)
        grp_rows.append(m1 + m2)
    grp = jnp.concatenate(grp_rows, axis=0)
    iota_n = lax.broadcasted_iota(jnp.int32, (N_GROUPS, tt), 0)
    gsel = jnp.zeros((N_GROUPS, tt), jnp.int32)
    for _ in range(TOPK_GROUPS):
        ig, _m = first_argmax(grp, iota_n, N_GROUPS)
        hit = iota_n == ig
        gsel = jnp.where(hit, 1, gsel)
        grp = jnp.where(hit, neg, grp)
    masked = jnp.concatenate(
        [jnp.where(gsel[g:g + 1, :] > 0, biased[g * gsz:(g + 1) * gsz, :], neg) for g in range(N_GROUPS)],
        axis=0)
    iota_e = lax.broadcasted_iota(jnp.int32, (n_exp, tt), 0)
    idx_rows, gate_rows = [], []
    for _ in range(TOP_K):
        ie, _m = first_argmax(masked, iota_e, n_exp)
        hit = iota_e == ie
        idx_rows.append(ie)
        gate_rows.append(jnp.sum(jnp.where(hit, scores, 0.0), axis=0, keepdims=True))
        masked = jnp.where(hit, neg, masked)
    pad = SUBLANES - TOP_K
    gsum = sum(gate_rows)
    idx_ref[0] = jnp.concatenate(idx_rows + [jnp.zeros((pad, tt), jnp.int32)], axis=0)
    gates_ref[0] = jnp.concatenate([gr / gsum * ROUTED_SCALE for gr in gate_rows]
                                   + [jnp.zeros((pad, tt), F32)], axis=0)


def _post_attn_router(x, o_lat, mod3, g_post, g_pre, router_wt, router_b, tt):
    b, s, d = x.shape
    n_exp = router_wt.shape[0]
    nblk = s // tt
    row = pl.BlockSpec((1, tt, d), lambda i, j: (i, j, 0))
    modspec = lambda chunk: pl.BlockSpec((1, 1, d), lambda i, j: (i, 0, chunk))
    vec = pl.BlockSpec((1, d), lambda i, j: (0, 0))
    sel = pl.BlockSpec((1, SUBLANES, tt), lambda i, j: (i * nblk + j, 0, 0))
    return pl.pallas_call(
        functools.partial(_post_attn_kernel, n_exp=n_exp),
        out_shape=[jax.ShapeDtypeStruct((b, s, d), F32), jax.ShapeDtypeStruct((b, s, d), F32),
                   jax.ShapeDtypeStruct((b * nblk, SUBLANES, tt), jnp.int32),
                   jax.ShapeDtypeStruct((b * nblk, SUBLANES, tt), F32)],
        grid=(b, nblk),
        in_specs=[row, row, modspec(2), modspec(3), modspec(4), vec, vec,
                  pl.BlockSpec((n_exp, d), lambda i, j: (0, 0)),
                  pl.BlockSpec((n_exp, 1), lambda i, j: (0, 0))],
        out_specs=[row, row, sel, sel],
        compiler_params=_params("parallel", "parallel"),
        name="post_attn_router",
    )(x, o_lat, mod3, mod3, mod3, g_post.reshape(1, d), g_pre.reshape(1, d), router_wt,
      router_b.reshape(n_exp, 1))


def _expert_kernel(blk_e_ref, tok_hbm, dst_hbm, wgt_ref, h_hbm, w1_ref, w3_ref, w2_ref, out_hbm,
                   tok_smem, dst_smem, xbuf, ybuf, sem_idx, sem_in, sem_out, *, rows):
    i = pl.program_id(0)
    c_tok = pltpu.make_async_copy(tok_hbm.at[i], tok_smem, sem_idx.at[0])
    c_dst = pltpu.make_async_copy(dst_hbm.at[i], dst_smem, sem_idx.at[1])
    c_tok.start()
    c_dst.start()
    c_tok.wait()
    c_dst.wait()

    def row_in(r):
        return pltpu.make_async_copy(h_hbm.at[tok_smem[r]], xbuf.at[r], sem_in)

    def row_out(r):
        return pltpu.make_async_copy(ybuf.at[r], out_hbm.at[dst_smem[r]], sem_out)

    def start_in(r, carry):
        row_in(r).start()
        return carry

    def wait_in(r, carry):
        row_in(r).wait()
        return carry

    lax.fori_loop(0, rows, start_in, 0)
    lax.fori_loop(0, rows, wait_in, 0)

    xb = xbuf[...].astype(BF16)
    hb = _dot(xb, w1_ref[0])
    hb = hb * _sigmoid(hb) * _dot(xb, w3_ref[0])
    ybuf[...] = (_dot(hb.astype(BF16), w2_ref[0]) * wgt_ref[0]).astype(ybuf.dtype)

    def start_out(r, carry):
        @pl.when(dst_smem[r] >= 0)
        def _():
            row_out(r).start()
        return carry

    def wait_out(r, carry):
        @pl.when(dst_smem[r] >= 0)
        def _():
            row_out(r).wait()
        return carry

    lax.fori_loop(0, rows, start_out, 0)
    lax.fori_loop(0, rows, wait_out, 0)


def _routed_experts(blk_expert, tok_buf, dst_buf, w_buf, h2, w1, w3, w2, n_slots):
    nb, rows = tok_buf.shape
    t, d = h2.shape
    ff = w1.shape[-1]
    grid_spec = pltpu.PrefetchScalarGridSpec(
        num_scalar_prefetch=1,
        grid=(nb,),
        in_specs=[pl.BlockSpec(memory_space=pl.ANY),
                  pl.BlockSpec(memory_space=pl.ANY),
                  pl.BlockSpec((1, rows, 1), lambda i, be: (i, 0, 0)),
                  pl.BlockSpec(memory_space=pl.ANY),
                  pl.BlockSpec((1, d, ff), lambda i, be: (be[i], 0, 0)),
                  pl.BlockSpec((1, d, ff), lambda i, be: (be[i], 0, 0)),
                  pl.BlockSpec((1, ff, d), lambda i, be: (be[i], 0, 0))],
        out_specs=pl.BlockSpec(memory_space=pl.ANY),
        scratch_shapes=[pltpu.SMEM((rows,), jnp.int32), pltpu.SMEM((rows,), jnp.int32),
                        pltpu.VMEM((rows, d), F32), pltpu.VMEM((rows, d), F32),
                        pltpu.SemaphoreType.DMA((2,)), pltpu.SemaphoreType.DMA, pltpu.SemaphoreType.DMA],
    )
    return pl.pallas_call(
        functools.partial(_expert_kernel, rows=rows),
        out_shape=jax.ShapeDtypeStruct((n_slots, d), F32),
        grid_spec=grid_spec,
        compiler_params=_params("arbitrary"),
        name="routed_experts",
    )(blk_expert, tok_buf, dst_buf, w_buf.reshape(nb, rows, 1), h2, w1, w3, w2)


def _shared_kernel(h_ref, w1_ref, w3_ref, w2_ref, o_ref):
    hb = h_ref[...].astype(BF16)
    a = _dot(hb, w1_ref[...])
    a = a * _sigmoid(a) * _dot(hb, w3_ref[...])
    o_ref[...] = _dot(a.astype(BF16), w2_ref[...]).astype(o_ref.dtype)


def _shared_expert(h2, sw1, sw3, sw2):
    t, d = h2.shape
    ff = sw1.shape[1]
    tm = _tile(t, 512, 16)
    full = lambda shape: pl.BlockSpec(shape, lambda i: (0,) * len(shape))
    return pl.pallas_call(
        _shared_kernel,
        out_shape=jax.ShapeDtypeStruct((t, d), BF16),
        grid=(t // tm,),
        in_specs=[pl.BlockSpec((tm, d), lambda i: (i, 0)), full((d, ff)), full((d, ff)), full((ff, d))],
        out_specs=pl.BlockSpec((tm, d), lambda i: (i, 0)),
        compiler_params=_params("parallel"),
        name="shared_expert",
    )(h2, sw1, sw3, sw2)


def _combine_kernel(routed_ref, shared_ref, xn_ref, gate_ref, g_ref, o_ref, *, top_k):
    y = shared_ref[0].astype(F32)
    d = y.shape[1]
    for k in range(top_k):
        y = y + routed_ref[0, :, k * d:(k + 1) * d]
    o_ref[0] = xn_ref[0] + gate_ref[0] * _rms(y, g_ref[...])


def _combine(routed, shared, xn, mod3, g_post, tt):
    b, s, d = xn.shape
    row = pl.BlockSpec((1, tt, d), lambda i, j: (i, j, 0))
    return pl.pallas_call(
        functools.partial(_combine_kernel, top_k=TOP_K),
        out_shape=jax.ShapeDtypeStruct((b, s, d), F32),
        grid=(b, s // tt),
        in_specs=[pl.BlockSpec((1, tt, TOP_K * d), lambda i, j: (i, j, 0)), row, row,
                  pl.BlockSpec((1, 1, d), lambda i, j: (i, 0, 5)),
                  pl.BlockSpec((1, d), lambda i, j: (0, 0))],
        out_specs=row,
        compiler_params=_params("parallel", "parallel"),
        name="moe_combine",
    )(routed, shared, xn, mod3, g_post.reshape(1, d))


def _rope_tables(b, rows, n_ctx_rows, q_scale):
    row = jnp.repeat(jnp.arange(rows, dtype=F32), GRID_W)
    col = jnp.tile(jnp.arange(GRID_W, dtype=F32), rows)
    n_pairs = DA_HEAD_DIM // 4
    inv = jnp.power(jnp.float32(ROPE_THETA), -jnp.arange(n_pairs, dtype=F32) / n_pairs)
    ang = jnp.concatenate([row[:, None] * inv, col[:, None] * inv], axis=-1)
    cos, sin = jnp.cos(ang), jnp.sin(ang)
    cos = jnp.concatenate([jnp.ones((n_ctx_rows, cos.shape[1]), F32), cos], axis=0)
    sin = jnp.concatenate([jnp.zeros((n_ctx_rows, sin.shape[1]), F32), sin], axis=0)
    cfull = jnp.tile(jnp.concatenate([cos, cos], axis=1), (b, 1))
    sfull = jnp.tile(jnp.concatenate([-sin, sin], axis=1), (b, 1))
    return (cfull * q_scale, sfull * q_scale), (cfull, sfull)


def _deinterleave_perm(width):
    idx = np.arange(width).reshape(-1, DA_HEAD_DIM // 2, 2)
    return np.concatenate([idx[..., 0], idx[..., 1]], axis=1).reshape(-1)


def _dispatch(eidx, gates, n_exp, rows):
    t = eidx.shape[0]
    tk = t * TOP_K
    nb = -(-tk // rows) + n_exp
    e_flat = eidx.reshape(-1)
    order = jnp.argsort(e_flat)
    e_s = e_flat[order]
    counts = jnp.bincount(e_flat, length=n_exp).astype(jnp.int32)
    starts = jnp.cumsum(counts) - counts
    padded = ((counts + rows - 1) // rows) * rows
    pends = jnp.cumsum(padded)
    pstarts = pends - padded
    dest = pstarts[e_s] + (jnp.arange(tk, dtype=jnp.int32) - starts[e_s])
    tok_buf = jnp.zeros((nb * rows,), jnp.int32).at[dest].set((order // TOP_K).astype(jnp.int32))
    dst_buf = jnp.full((nb * rows,), -1, jnp.int32).at[dest].set(order.astype(jnp.int32))
    w_buf = jnp.zeros((nb * rows,), F32).at[dest].set(gates.reshape(-1)[order])
    blk_expert = jnp.clip(jnp.searchsorted(pends, jnp.arange(nb, dtype=jnp.int32) * rows, side='right'),
                          0, n_exp - 1).astype(jnp.int32)
    return blk_expert, tok_buf.reshape(nb, rows), dst_buf.reshape(nb, rows), w_buf.reshape(nb, rows)


def kernel(x, c, ctx, c_ctx, w_mod, b_mod, g_pre_attn, g_post_attn, g_pre_ffn, g_post_ffn, w_in, w_out, da_lambda, da_subln, rw_shift, rw_k_k, rw_k_a, rw_r_k, rw_w0, rw_w2, rw_a0, rw_a2, rw_g2, rw_ln_w, rw_ln_b, router_w, router_bias, exp_w1, exp_w3, exp_w2, sh_w1, sh_w3, sh_w2):
    assert w_mod.shape[0] == 1, "single-layer trunk only"
    b, s, d = x.shape
    n_ctx = ctx.shape[1]
    lc = n_ctx + s
    da_w = d // 2
    rw = d - da_w
    lora_w, lora_a, lora_g = rw_w2.shape[2], rw_a2.shape[2], rw_g2.shape[1]
    n_exp = router_w.shape[-1]
    tt = _tile(math.gcd(n_ctx, s), 256, 16)
    tf = _tile(math.gcd(n_ctx, s), 128, 16)

    cv = jnp.concatenate([c, c_ctx[None], jnp.zeros((SUBLANES - b - 1, d), c.dtype)], axis=0)
    mod = _modulation(cv, w_mod[0], b_mod[0])
    mod3 = mod.reshape(SUBLANES, 1, N_MOD * d)

    u = _prenorm(ctx, x, g_pre_attn[0], mod3, tt).reshape(b * lc, d)

    w_in0 = w_in[0]
    perm = _deinterleave_perm(da_w)
    w_q = w_in0[:, :da_w][:, perm].astype(BF16)
    w_k = w_in0[:, da_w:2 * da_w][:, perm].astype(BF16)
    w_v = w_in0[:, 2 * da_w:3 * da_w].astype(BF16)
    lora_gp = -(-lora_g // LANES) * LANES
    w_rw = jnp.pad(w_in0[:, 3 * da_w:], ((0, 0), (0, lora_gp - lora_g))).astype(BF16)
    rope_q, rope_k = _rope_tables(b, s // GRID_W, n_ctx, DA_HEAD_DIM ** -0.5)
    q = _matmul(u, w_q, rope_q).reshape(b, lc, da_w)
    k = _matmul(u, w_k, rope_k).reshape(b, lc, da_w)
    v = _matmul(u, w_v).reshape(b, lc, da_w)
    p_rw = _matmul(u, w_rw).reshape(b, lc, -1)

    lam_init = 0.8 - 0.6 * math.exp(-0.3 * 0)
    lv = da_lambda[0].astype(F32)
    lam = (jnp.exp(jnp.sum(lv[0] * lv[1])) - jnp.exp(jnp.sum(lv[2] * lv[3])) + lam_init).reshape(1)
    g_sub = (da_subln[0] * (1.0 - lam_init)).reshape(1, -1)
    o_da = _diff_attention(lam, q, k, v, g_sub, n_ctx, tt)

    mu = jnp.pad(rw_shift[0], ((0, 0), (0, lora_gp - lora_g)))
    vecs = jnp.stack([rw_k_k[0], rw_k_a[0], rw_r_k[0].reshape(-1)])
    g2p = jnp.pad(rw_g2[0], ((0, lora_gp - lora_g), (0, 0))).astype(BF16)
    r, vr, kk, lw0, lw1, b0, b1, kd0, kd1, bonus, gate = _rw_features(
        p_rw, mu, vecs, rw_w0[0], rw_a0[0], rw_w2[0].astype(BF16), rw_a2[0].astype(BF16), g2p,
        rw, n_ctx, tf)
    y_f = _rw_scan(r, vr, kk, lw0, b0, kd0, n_ctx, reverse=False)
    y_b = _rw_scan(r, vr, kk, lw1, b1, kd1, n_ctx, reverse=True)
    o_rw = _rw_output(y_f, y_b, bonus, gate, jnp.stack([rw_ln_w[0], rw_ln_b[0]]), n_ctx, tt)

    o_lat = _matmul2(o_da.reshape(b * s, da_w), o_rw.reshape(b * s, rw), w_out[0].astype(BF16))
    x_new, h, eidx, gates = _post_attn_router(
        x, o_lat.reshape(b, s, d), mod3, g_post_attn[0], g_pre_ffn[0], router_w[0].T, router_bias[0], tf)
    eidx = jnp.swapaxes(eidx[:, :TOP_K, :], 1, 2).reshape(b * s, TOP_K)
    gates = jnp.swapaxes(gates[:, :TOP_K, :], 1, 2).reshape(b * s, TOP_K)

    blk_expert, tok_buf, dst_buf, w_buf = _dispatch(eidx, gates, n_exp, EXPERT_ROWS)
    routed = _routed_experts(blk_expert, tok_buf, dst_buf, w_buf, h.reshape(b * s, d),
                             exp_w1[0].astype(BF16), exp_w3[0].astype(BF16), exp_w2[0].astype(BF16),
                             b * s * TOP_K)
    shared = _shared_expert(h.reshape(b * s, d), sh_w1[0].astype(BF16), sh_w3[0].astype(BF16),
                            sh_w2[0].astype(BF16))
    return _combine(routed.reshape(b, s, TOP_K * d), shared.reshape(b, s, d), x_new, mod3,
                    g_post_ffn[0], tf)
```
